```python
import jax, jax.numpy as jnp
from jax import lax
import numpy as np

D_MODEL = 1024
BATCH = 8
SEQ = 2048
DEPTH = 4
DEC_BATCH = 128
DEC_SEQ = 1
PAST_LEN = 2048
PAGE_SIZE = 128

A_HEADS = 16
A_HEAD_DIM = 128
A_KV_HEADS = 4
A_WIDTH = A_HEADS * A_HEAD_DIM
A_KV_WIDTH = A_KV_HEADS * A_HEAD_DIM
IDX_HEADS = 8
IDX_DIM = 64
TOPK_MAX = 256
Q_BLOCK = 64
A_SPLITS = (A_WIDTH, A_KV_WIDTH, A_KV_WIDTH, A_WIDTH, IDX_HEADS * IDX_DIM, IDX_DIM, IDX_HEADS)
A_IN = sum(A_SPLITS)
R_HEADS = 4
R_KEY_DIM = 256
R_VAL_DIM = 512
R_QK_WIDTH = R_HEADS * R_KEY_DIM
R_WIDTH = R_HEADS * R_VAL_DIM
R_CHUNK = 128
R_SPLITS = (R_QK_WIDTH, R_QK_WIDTH, R_WIDTH, R_WIDTH)
R_IN = sum(R_SPLITS)
ROPE_BASE = 10000.0
EPS = 1e-6
N_ATTN_LAYERS = (DEPTH + 1) // 2
N_RET_LAYERS = DEPTH // 2

kernel_name = 'dsa_retention_hybrid_step'


def _split(h, sizes):
    offs = [int(o) for o in np.cumsum(sizes)[:-1]]
    return jnp.split(h, offs, axis=-1)


def rmsnorm(x, w):
    xf = x.astype(jnp.float32)
    y = xf * lax.rsqrt(jnp.mean(xf * xf, axis=-1, keepdims=True) + EPS)
    return (y * w.astype(jnp.float32)).astype(x.dtype)


def index_topk(q_idx, w_idx, k_idx, q_pos, k_pos, n_sel):
    dots = jnp.einsum('bthd,bsd->bths', q_idx, k_idx, preferred_element_type=jnp.float32) * (IDX_DIM ** -0.5)
    score = jnp.einsum('bths,bth->bts', jax.nn.relu(dots), w_idx.astype(jnp.float32)) * (IDX_HEADS ** -0.5)
    score = jnp.where(k_pos[None, None, :] <= q_pos[None, :, None], score, -jnp.inf)
    _, sel = lax.top_k(score, n_sel)
    return sel


def gather_rows(rows, idx):
    B, T, K = idx.shape
    g = jax.vmap(lambda r, i: r[i])(rows, idx.reshape(B, T * K))
    return g.reshape(B, T, K, *rows.shape[2:])


def sparse_attend(q, k_sel, v_sel, valid):
    B, T = q.shape[:2]
    qg = q.reshape(B, T, A_KV_HEADS, A_HEADS // A_KV_HEADS, A_HEAD_DIM)
    s = jnp.einsum('btkgd,btnkd->btkgn', qg, k_sel, preferred_element_type=jnp.float32) * (A_HEAD_DIM ** -0.5)
    s = jnp.where(valid[:, :, None, None, :], s, -jnp.inf)
    p = jax.nn.softmax(s, axis=-1).astype(v_sel.dtype)
    o = jnp.einsum('btkgn,btnkd->btkgd', p, v_sel)
    return o.reshape(B, T, A_WIDTH)


def attn_project(h, w_in):
    B, T, _ = h.shape
    q, k, v, g, qi, ki, wi = _split(h @ w_in, A_SPLITS)
    q = q.reshape(B, T, A_HEADS, A_HEAD_DIM)
    k = k.reshape(B, T, A_KV_HEADS, A_HEAD_DIM)
    v = v.reshape(B, T, A_KV_HEADS, A_HEAD_DIM)
    qi = qi.reshape(B, T, IDX_HEADS, IDX_DIM)
    return q, k, v, g, qi, ki, wi


def attn_prompt(h, w_in, w_out):
    B, S, _ = h.shape
    q, k, v, g, qi, ki, wi = attn_project(h, w_in)
    n_sel = min(TOPK_MAX, S // 4)
    nb = S // Q_BLOCK
    pos = jnp.arange(S, dtype=jnp.int32)

    def blocks(a):
        return jnp.swapaxes(a.reshape(B, nb, Q_BLOCK, *a.shape[2:]), 0, 1)

    def one_block(args):
        qb, qib, wib, pb = args
        sel = index_topk(qib, wib, ki, pb, pos, n_sel)
        valid = sel <= pb[None, :, None]
        return sparse_attend(qb, gather_rows(k, sel), gather_rows(v, sel), valid)

    o = lax.map(one_block, (blocks(q), blocks(qi), blocks(wi), pos.reshape(nb, Q_BLOCK)))
    o = jnp.swapaxes(o, 0, 1).reshape(B, S, A_WIDTH)
    y = (o * jax.nn.silu(g)) @ w_out
    return y, k, v, ki


def attn_sample(h, cache_k, cache_v, cache_ki, page_table, w_in, w_out):
    Bd, T, _ = h.shape
    q, k, v, g, qi, ki, wi = attn_project(h, w_in)
    n_pages = page_table.shape[1]
    past = n_pages * PAGE_SIZE
    L = past + T
    n_sel = min(TOPK_MAX, L // 4)
    ki_past = cache_ki[page_table].reshape(Bd, past, IDX_DIM)
    ki_all = jnp.concatenate([ki_past, ki], axis=1)
    q_pos = past + jnp.arange(T, dtype=jnp.int32)
    k_pos = jnp.arange(L, dtype=jnp.int32)
    sel = index_topk(qi, wi, ki_all, q_pos, k_pos, n_sel)
    valid = sel <= q_pos[None, :, None]
    flat = sel.reshape(Bd, T * n_sel)
    in_past = flat < past
    ps = jnp.minimum(flat, past - 1)
    phys = jnp.take_along_axis(page_table, ps // PAGE_SIZE, axis=1)
    off = ps % PAGE_SIZE
    new_i = jnp.clip(flat - past, 0, T - 1)

    def pick(pool, new):
        old = pool[phys, off]
        cur = jax.vmap(lambda r, i: r[i])(new, new_i)
        return jnp.where(in_past[:, :, None, None], old, cur).reshape(Bd, T, n_sel, A_KV_HEADS, A_HEAD_DIM)

    o = sparse_attend(q, pick(cache_k, k), pick(cache_v, v), valid)
    y = (o * jax.nn.silu(g)) @ w_out
    return y, k, v, ki


def retention_log_decay():
    return jnp.log1p(-jnp.exp2(-5.0 - jnp.arange(R_HEADS, dtype=jnp.float32)))


def rotary(x, pos):
    half = R_KEY_DIM // 2
    theta = 1.0 / (ROPE_BASE ** jnp.linspace(0.0, 1.0, half, dtype=jnp.float32))
    ang = pos.astype(jnp.float32)[:, None] * theta[None, :]
    cos = jnp.repeat(jnp.cos(ang), 2, axis=-1)[None, :, None, :]
    sin = jnp.repeat(jnp.sin(ang), 2, axis=-1)[None, :, None, :]
    x1 = x[..., 0::2]
    x2 = x[..., 1::2]
    rot = jnp.stack([-x2, x1], axis=-1).reshape(x.shape)
    return (x * cos + rot * sin).astype(x.dtype)


def retention_chunk(S, q, k, v):
    C = q.shape[1]
    lg = retention_log_decay()
    i = jnp.arange(C, dtype=jnp.float32)
    diff = i[:, None] - i[None, :]
    dec = jnp.exp(jnp.where(diff[None] >= 0, diff[None] * lg[:, None, None], -jnp.inf))
    qk = jnp.einsum('bihd,bjhd->bhij', q, k, preferred_element_type=jnp.float32) * dec[None]
    inner = jnp.einsum('bhij,bjhe->bihe', qk, v.astype(jnp.float32))
    cross = jnp.einsum('bihd,bhde->bihe', q.astype(jnp.float32), S) * jnp.exp((i[:, None] + 1.0) * lg[None, :])[None, :, :, None]
    kd = k.astype(jnp.float32) * jnp.exp((C - 1.0 - i)[:, None] * lg[None, :])[None, :, :, None]
    S_new = jnp.exp(C * lg)[None, :, None, None] * S + jnp.einsum('bjhd,bjhe->bhde', kd, v.astype(jnp.float32))
    return S_new, inner + cross


def retention_project(h, w_in, pos):
    B, T, _ = h.shape
    q, k, v, g = _split(h @ w_in, R_SPLITS)
    q = rotary(q.reshape(B, T, R_HEADS, R_KEY_DIM), pos)
    k = rotary(k.reshape(B, T, R_HEADS, R_KEY_DIM), pos) * (R_KEY_DIM ** -0.5)
    v = v.reshape(B, T, R_HEADS, R_VAL_DIM)
    return q, k, v, g


def retention_output(o, g, gn_w, w_out):
    B, T = o.shape[:2]
    mu = jnp.mean(o, axis=-1, keepdims=True)
    var = jnp.mean(jnp.square(o - mu), axis=-1, keepdims=True)
    o = (o - mu) * lax.rsqrt(var + EPS) * gn_w.astype(jnp.float32)[None, None]
    o = o.reshape(B, T, R_WIDTH).astype(g.dtype) * jax.nn.silu(g)
    return o @ w_out


def ret_prompt(h, w_in, gn_w, w_out):
    B, S, _ = h.shape
    q, k, v, g = retention_project(h, w_in, jnp.arange(S, dtype=jnp.int32))
    nc = S // R_CHUNK

    def chunks(a):
        return jnp.swapaxes(a.reshape(B, nc, R_CHUNK, *a.shape[2:]), 0, 1)

    S0 = jnp.zeros((B, R_HEADS, R_KEY_DIM, R_VAL_DIM), jnp.float32)
    S_fin, o = lax.scan(lambda s, xs: retention_chunk(s, *xs), S0, (chunks(q), chunks(k), chunks(v)))
    o = jnp.swapaxes(o, 0, 1).reshape(B, S, R_HEADS, R_VAL_DIM)
    return retention_output(o, g, gn_w, w_out), S_fin


def ret_sample(h, state, w_in, gn_w, w_out):
    T = h.shape[1]
    q, k, v, g = retention_project(h, w_in, PAST_LEN + jnp.arange(T, dtype=jnp.int32))
    S_new, o = retention_chunk(state.astype(jnp.float32), q, k, v)
    return retention_output(o, g, gn_w, w_out), S_new


def setup_inputs(seed: int = 0) -> dict:
    key = jax.random.key(seed)
    ks = jax.random.split(key, 16)
    n_pages = PAST_LEN // PAGE_SIZE
    n_used = DEC_BATCH * n_pages
    n_pool = n_used + (n_used + 3) // 4
    perm = jax.random.permutation(ks[0], n_pool)
    page_table = perm[:n_used].reshape(DEC_BATCH, n_pages).astype(jnp.int32)
    nrm = jax.random.normal
    f32 = jnp.float32
    return {
        'x_prompt': nrm(ks[1], (BATCH, SEQ, D_MODEL), f32),
        'x_sample': nrm(ks[2], (DEC_BATCH, DEC_SEQ, D_MODEL), f32),
        'cache_k': nrm(ks[3], (N_ATTN_LAYERS, n_pool, PAGE_SIZE, A_KV_HEADS, A_HEAD_DIM), f32),
        'cache_v': nrm(ks[4], (N_ATTN_LAYERS, n_pool, PAGE_SIZE, A_KV_HEADS, A_HEAD_DIM), f32),
        'cache_kidx': nrm(ks[5], (N_ATTN_LAYERS, n_pool, PAGE_SIZE, IDX_DIM), f32),
        'state_ret': nrm(ks[6], (N_RET_LAYERS, DEC_BATCH, R_HEADS, R_KEY_DIM, R_VAL_DIM), f32),
        'page_table': page_table,
        'norm_w': 1.0 + 0.1 * nrm(ks[7], (DEPTH, D_MODEL), f32),
        'w_in_attn': nrm(ks[8], (N_ATTN_LAYERS, D_MODEL, A_IN), f32) * D_MODEL ** -0.5,
        'w_out_attn': nrm(ks[9], (N_ATTN_LAYERS, A_WIDTH, D_MODEL), f32) * A_WIDTH ** -0.5,
        'w_in_ret': nrm(ks[10], (N_RET_LAYERS, D_MODEL, R_IN), f32) * D_MODEL ** -0.5,
        'w_out_ret': nrm(ks[11], (N_RET_LAYERS, R_WIDTH, D_MODEL), f32) * R_WIDTH ** -0.5,
        'gn_w_ret': 1.0 + 0.1 * nrm(ks[12], (N_RET_LAYERS, R_HEADS, R_VAL_DIM), f32),
        'final_norm_w': 1.0 + 0.1 * nrm(ks[13], (D_MODEL,), f32),
    }


def reference(x_prompt, x_sample, cache_k, cache_v, cache_kidx, state_ret, page_table,
              norm_w, w_in_attn, w_out_attn, w_in_ret, w_out_ret, gn_w_ret, final_norm_w):
    xp = x_prompt
    xs = x_sample
    kp, vp, kip, ksm, vsm, kis, srp, srs = [], [], [], [], [], [], [], []
    for i in range(DEPTH):
        hp = rmsnorm(xp, norm_w[i])
        hs = rmsnorm(xs, norm_w[i])
        j = i // 2
        if i % 2 == 0:
            yp, k_p, v_p, ki_p = attn_prompt(hp, w_in_attn[j], w_out_attn[j])
            ys, k_s, v_s, ki_s = attn_sample(hs, cache_k[j], cache_v[j], cache_kidx[j], page_table,
                                             w_in_attn[j], w_out_attn[j])
            kp.append(k_p); vp.append(v_p); kip.append(ki_p)
            ksm.append(k_s); vsm.append(v_s); kis.append(ki_s)
        else:
            yp, s_p = ret_prompt(hp, w_in_ret[j], gn_w_ret[j], w_out_ret[j])
            ys, s_s = ret_sample(hs, state_ret[j], w_in_ret[j], gn_w_ret[j], w_out_ret[j])
            srp.append(s_p.astype(state_ret.dtype)); srs.append(s_s.astype(state_ret.dtype))
        xp = xp + yp
        xs = xs + ys
    y_prompt = rmsnorm(xp, final_norm_w)
    y_sample = rmsnorm(xs, final_norm_w)
    return (y_prompt, y_sample, jnp.stack(kp), jnp.stack(vp), jnp.stack(kip),
            jnp.stack(ksm), jnp.stack(vsm), jnp.stack(kis), jnp.stack(srp), jnp.stack(srs))
```

```python
import functools

import numpy as np
import jax
import jax.numpy as jnp
from jax import lax
from jax.experimental import pallas as pl
from jax.experimental.pallas import tpu as pltpu

F32 = jnp.float32
BF16 = jnp.bfloat16
I32 = jnp.int32

D_MODEL = 1024
DEPTH = 4
PAGE_SIZE = 128
EPS = 1e-6
ROPE_BASE = 10000.0

A_HEADS = 16
A_HEAD_DIM = 128
A_KV_HEADS = 4
A_GROUP = A_HEADS // A_KV_HEADS
A_WIDTH = A_HEADS * A_HEAD_DIM
A_KV_WIDTH = A_KV_HEADS * A_HEAD_DIM
IDX_HEADS = 8
IDX_DIM = 64
IDX_WIDTH = IDX_HEADS * IDX_DIM
TOPK_MAX = 256
A_IN = 2 * A_WIDTH + 2 * A_KV_WIDTH + IDX_WIDTH + IDX_DIM + IDX_HEADS
OFF_Q = 0
OFF_K = A_WIDTH
OFF_V = OFF_K + A_KV_WIDTH
OFF_G = OFF_V + A_KV_WIDTH
OFF_QI = OFF_G + A_WIDTH
OFF_KIWI = OFF_QI + IDX_WIDTH
LANES = 128
A_IN_PAD = OFF_KIWI + LANES

R_HEADS = 4
R_KEY_DIM = 256
R_VAL_DIM = 512
R_QK_WIDTH = R_HEADS * R_KEY_DIM
R_WIDTH = R_HEADS * R_VAL_DIM
R_CHUNK = 128
R_IN = 2 * R_QK_WIDTH + 2 * R_WIDTH

INT_MIN = -2 ** 31
NEG_INF_KEY = -2139095041
VMEM_LIMIT = 56 * 1024 * 1024


def _dot(a, b):
    return jnp.dot(a, b, preferred_element_type=F32)


def _dot_nt(a, b):
    return lax.dot_general(a, b, (((1,), (1,)), ((), ())), preferred_element_type=F32)


def _silu(g):
    return g / (1.0 + jnp.exp(-g))


def _sortable(x):
    i = lax.bitcast_convert_type(x + 0.0, I32)
    return i ^ ((i >> 31) & 0x7FFFFFFF)


def _params(sem, vmem=VMEM_LIMIT):
    return pltpu.CompilerParams(dimension_semantics=sem, vmem_limit_bytes=vmem)


def _count_rows(keys_ref, n_act, rows, width, pred):
    nt = width // LANES

    def body(c, acc):
        for t in range(nt):
            x = keys_ref[c, :, t * LANES:(t + 1) * LANES]
            acc = acc + jnp.where(pred(x, c * width + t * LANES), 1.0, 0.0)
        return acc

    acc = lax.fori_loop(0, n_act, body, jnp.zeros((rows, LANES), F32))
    return jnp.sum(acc, axis=-1, keepdims=True)


def _kth_largest(keys_ref, n_act, rows, width, k):
    kf = float(k)

    def count_ge(cand):
        cb = jnp.broadcast_to(cand, (rows, LANES))
        return _count_rows(keys_ref, n_act, rows, width, lambda x, i0: x >= cb)

    base0 = jnp.where(count_ge(jnp.zeros((rows, 1), I32)) >= kf, 0, INT_MIN).astype(I32)

    def bit_body(i, base):
        cand = base | jnp.left_shift(jnp.int32(1), 30 - i)
        return jnp.where(count_ge(cand) >= kf, cand, base)

    return lax.fori_loop(0, 31, bit_body, base0)


def _tie_cut(keys_ref, n_act, rows, width, thr, need, idx_bits):
    tb = jnp.broadcast_to(thr, (rows, LANES))
    lane = lax.broadcasted_iota(I32, (rows, LANES), 1)

    def bit_body(i, m):
        cand = m | jnp.left_shift(jnp.int32(1), idx_bits - 1 - i)
        cb = jnp.broadcast_to(cand, (rows, LANES))
        cnt = _count_rows(keys_ref, n_act, rows, width,
                          lambda x, i0: (x == tb) & (lane + i0 < cb))
        return jnp.where(cnt < need, cand, m)

    return lax.fori_loop(0, idx_bits, bit_body, jnp.zeros((rows, 1), I32))


def _select(keys_ref, cut_ref, n_act, rows, width, k, idx_bits):
    thr = _kth_largest(keys_ref, n_act, rows, width, k)
    tb = jnp.broadcast_to(thr, (rows, LANES))
    cnt_ge = _count_rows(keys_ref, n_act, rows, width, lambda x, i0: x >= tb)
    cut_ref[...] = jnp.full((rows, 1), 2 ** idx_bits, I32)

    @pl.when(jnp.max(cnt_ge) > float(k))
    def _():
        cnt_gt = _count_rows(keys_ref, n_act, rows, width, lambda x, i0: x > tb)
        cut_ref[...] = _tie_cut(keys_ref, n_act, rows, width, thr, float(k) - cnt_gt, idx_bits)

    return thr, cut_ref[...]


def _rms(x, w):
    return x * lax.rsqrt(jnp.mean(x * x, axis=-1, keepdims=True) + EPS) * w


def _attn_proj_kernel(x_ref, nw_ref, w_ref, q_ref, k_ref, v_ref, kb_ref, vb_ref, g_ref, qi_ref, kiwi_ref):
    h = _rms(x_ref[...], nw_ref[...]).astype(BF16)
    q_ref[...] = (_dot(h, w_ref[:, OFF_Q:OFF_K]) * (A_HEAD_DIM ** -0.5)).astype(q_ref.dtype)
    k = _dot(h, w_ref[:, OFF_K:OFF_V])
    k_ref[...] = k
    kb_ref[...] = k.astype(BF16)
    v = _dot(h, w_ref[:, OFF_V:OFF_G])
    v_ref[...] = v
    vb_ref[...] = v.astype(BF16)
    g_ref[...] = _dot(h, w_ref[:, OFF_G:OFF_QI])
    qi_ref[...] = _dot(h, w_ref[:, OFF_QI:OFF_KIWI]).astype(qi_ref.dtype)
    kiwi_ref[...] = _dot(h, w_ref[:, OFF_KIWI:A_IN_PAD])


def _attn_proj(x, nw, w, tm):
    m = x.shape[0]
    row = lambda n: pl.BlockSpec((tm, n), lambda i: (i, 0))
    outs = [(A_WIDTH, BF16), (A_KV_WIDTH, F32), (A_KV_WIDTH, F32), (A_KV_WIDTH, BF16), (A_KV_WIDTH, BF16),
            (A_WIDTH, F32), (IDX_WIDTH, BF16), (LANES, F32)]
    return pl.pallas_call(
        _attn_proj_kernel,
        grid=(m // tm,),
        in_specs=[row(D_MODEL), pl.BlockSpec((1, D_MODEL), lambda i: (0, 0)),
                  pl.BlockSpec((D_MODEL, A_IN_PAD), lambda i: (0, 0))],
        out_specs=[row(n) for n, _ in outs],
        out_shape=[jax.ShapeDtypeStruct((m, n), dt) for n, dt in outs],
        compiler_params=_params(("arbitrary",)),
        name="attn_proj",
    )(x, nw, w)


def _ret_proj_kernel(x_ref, nw_ref, w_ref, cos_ref, sin_ref, q_ref, k_ref, v_ref, g_ref):
    h = _rms(x_ref[...], nw_ref[...]).astype(BF16)
    cos = cos_ref[...]
    sin = sin_ref[...]
    even = (lax.broadcasted_iota(I32, cos.shape, 1) & 1) == 0
    for hd in range(R_HEADS):
        for off, ref, scale in ((0, q_ref, 1.0), (R_QK_WIDTH, k_ref, R_KEY_DIM ** -0.5)):
            z = _dot(h, w_ref[:, off + hd * R_KEY_DIM:off + (hd + 1) * R_KEY_DIM])
            z_next = pltpu.roll(z, R_KEY_DIM - 1, 1)
            z_prev = pltpu.roll(z, 1, 1)
            r = z * cos + jnp.where(even, z_next, z_prev) * sin
            ref[:, hd * R_KEY_DIM:(hd + 1) * R_KEY_DIM] = (r * scale).astype(ref.dtype)
    v_ref[...] = _dot(h, w_ref[:, 2 * R_QK_WIDTH:2 * R_QK_WIDTH + R_WIDTH]).astype(v_ref.dtype)
    g_ref[...] = _dot(h, w_ref[:, 2 * R_QK_WIDTH + R_WIDTH:R_IN])


def _ret_proj(x, nw, w, cos, sin, tm, out_dt):
    m = x.shape[0]
    npos = cos.shape[0] // tm
    row = lambda n: pl.BlockSpec((tm, n), lambda i: (i, 0))
    tab = pl.BlockSpec((tm, R_KEY_DIM), lambda i: (i % npos, 0))
    outs = [(R_QK_WIDTH, out_dt), (R_QK_WIDTH, out_dt), (R_WIDTH, out_dt), (R_WIDTH, F32)]
    return pl.pallas_call(
        _ret_proj_kernel,
        grid=(m // tm,),
        in_specs=[row(D_MODEL), pl.BlockSpec((1, D_MODEL), lambda i: (0, 0)),
                  pl.BlockSpec((D_MODEL, R_IN), lambda i: (0, 0)), tab, tab],
        out_specs=[row(n) for n, _ in outs],
        out_shape=[jax.ShapeDtypeStruct((m, n), dt) for n, dt in outs],
        compiler_params=_params(("arbitrary",)),
        name="ret_proj",
    )(x, nw, w, cos, sin)


def _out_proj_kernel(a_ref, w_ref, x_ref, o_ref):
    o_ref[...] = x_ref[...] + _dot(a_ref[...].astype(BF16), w_ref[...])


def _out_proj_norm_kernel(a_ref, w_ref, x_ref, fw_ref, o_ref):
    o_ref[...] = _rms(x_ref[...] + _dot(a_ref[...].astype(BF16), w_ref[...]), fw_ref[...])


def _out_proj(a, w, x, tm, final_w=None):
    m, width = a.shape
    row = lambda n: pl.BlockSpec((tm, n), lambda i: (i, 0))
    in_specs = [row(width), pl.BlockSpec((width, D_MODEL), lambda i: (0, 0)), row(D_MODEL)]
    args = [a, w, x]
    kern = _out_proj_kernel
    if final_w is not None:
        in_specs.append(pl.BlockSpec((1, D_MODEL), lambda i: (0, 0)))
        args.append(final_w)
        kern = _out_proj_norm_kernel
    return pl.pallas_call(
        kern,
        grid=(m // tm,),
        in_specs=in_specs,
        out_specs=row(D_MODEL),
        out_shape=jax.ShapeDtypeStruct((m, D_MODEL), F32),
        compiler_params=_params(("arbitrary",)),
        name="out_proj",
    )(*args)


def _attn_prompt_kernel(q_ref, g_ref, qi_ref, kiwi_ref, k_ref, v_ref, klo_ref, khi_ref, a_ref,
                        keys_ref, bias_ref, cut_ref, m_ref, l_ref, acc_ref, *, tq, n_sel):
    qt = pl.program_id(1)
    n_act = qt + 1
    n_chunks = keys_ref.shape[0]
    idx_bits = (n_chunks * tq).bit_length() - 1

    w = kiwi_ref[...] * (IDX_DIM ** -0.5 * IDX_HEADS ** -0.5)
    row = lax.broadcasted_iota(I32, (tq, tq), 0)
    col = lax.broadcasted_iota(I32, (tq, tq), 1)

    def score_chunk(c, carry):
        off = pl.multiple_of(c * tq, tq)
        klo = klo_ref[pl.ds(off, tq), :]
        khi = khi_ref[pl.ds(off, tq), :]
        acc = jnp.zeros((tq, tq), F32)
        for j in range(IDX_HEADS // 2):
            qj = qi_ref[:, j * LANES:(j + 1) * LANES]
            w0 = IDX_DIM + 2 * j
            acc = acc + jnp.maximum(_dot_nt(qj, klo), 0.0) * w[:, w0:w0 + 1]
            acc = acc + jnp.maximum(_dot_nt(qj, khi), 0.0) * w[:, w0 + 1:w0 + 2]
        first_future = row + jnp.where(c == qt, 0, tq)
        keys_ref[c] = jnp.where(col > first_future, NEG_INF_KEY, _sortable(acc))
        return carry

    lax.fori_loop(0, n_act, score_chunk, 0)

    thr, cut = _select(keys_ref, cut_ref, n_act, tq, tq, n_sel, idx_bits)

    def bias_chunk(c, carry):
        x = keys_ref[c]
        sel = (x > thr) | ((x == thr) & (col + c * tq <= cut))
        causal = col <= row + jnp.where(c == qt, 0, tq)
        bias_ref[c] = jnp.where(sel & causal, 0.0, -jnp.inf)
        return carry

    lax.fori_loop(0, n_act, bias_chunk, 0)

    for grp in range(A_KV_HEADS):
        lo = grp * A_HEAD_DIM
        q4 = jnp.concatenate(
            [q_ref[:, (grp * A_GROUP + i) * A_HEAD_DIM:(grp * A_GROUP + i + 1) * A_HEAD_DIM]
             for i in range(A_GROUP)], axis=0)
        m_ref[...] = jnp.full(m_ref.shape, -jnp.inf, F32)
        l_ref[...] = jnp.zeros(l_ref.shape, F32)
        acc_ref[...] = jnp.zeros(acc_ref.shape, F32)

        def kv_chunk(c, carry):
            off = pl.multiple_of(c * tq, tq)
            kc = k_ref[pl.ds(off, tq), lo:lo + A_HEAD_DIM]
            vc = v_ref[pl.ds(off, tq), lo:lo + A_HEAD_DIM]
            s = _dot_nt(q4, kc).reshape(A_GROUP, tq, tq) + bias_ref[c][None]
            m_old = m_ref[...]
            m_new = jnp.maximum(m_old, jnp.max(s, axis=-1, keepdims=True))
            m_safe = jnp.where(m_new == -jnp.inf, 0.0, m_new)
            p = jnp.exp(s - m_safe)
            alpha = jnp.exp(m_old - m_safe)
            l_ref[...] = alpha * l_ref[...] + jnp.sum(p, axis=-1, keepdims=True)
            pv = _dot(p.reshape(A_GROUP * tq, tq).astype(BF16), vc)
            acc_ref[...] = alpha * acc_ref[...] + pv.reshape(A_GROUP, tq, A_HEAD_DIM)
            m_ref[...] = m_new
            return carry

        lax.fori_loop(0, n_act, kv_chunk, 0)

        o = acc_ref[...] / l_ref[...]
        for i in range(A_GROUP):
            c0 = (grp * A_GROUP + i) * A_HEAD_DIM
            a_ref[:, c0:c0 + A_HEAD_DIM] = (o[i] * _silu(g_ref[:, c0:c0 + A_HEAD_DIM])).astype(a_ref.dtype)


def _attn_prompt(q, g, qi, kiwi, kb, vb, klo, khi, batch, seq, tq):
    nq = seq // tq
    n_sel = min(TOPK_MAX, seq // 4)
    assert tq >= n_sel, "the first key chunk must hold n_sel keys"
    qrow = lambda n: pl.BlockSpec((tq, n), lambda b, t: (b * nq + t, 0))
    full = lambda n: pl.BlockSpec((seq, n), lambda b, t: (b, 0))
    return pl.pallas_call(
        functools.partial(_attn_prompt_kernel, tq=tq, n_sel=n_sel),
        grid=(batch, nq),
        in_specs=[qrow(A_WIDTH), qrow(A_WIDTH), qrow(IDX_WIDTH), qrow(LANES),
                  full(A_KV_WIDTH), full(A_KV_WIDTH), full(LANES), full(LANES)],
        out_specs=qrow(A_WIDTH),
        out_shape=jax.ShapeDtypeStruct((batch * seq, A_WIDTH), BF16),
        scratch_shapes=[pltpu.VMEM((nq, tq, tq), I32), pltpu.VMEM((nq, tq, tq), F32),
                        pltpu.VMEM((tq, 1), I32),
                        pltpu.VMEM((A_GROUP, tq, 1), F32), pltpu.VMEM((A_GROUP, tq, 1), F32),
                        pltpu.VMEM((A_GROUP, tq, A_HEAD_DIM), F32)],
        compiler_params=_params(("arbitrary", "arbitrary")),
        name="attn_prompt",
    )(q, g, qi, kiwi, kb, vb, klo, khi)


def _idx_sample_kernel(pt_ref, qi_ref, w_ref, kiwi_ref, *refs, n_pages):
    pages = refs[:n_pages]
    out_ref = refs[n_pages]
    qi = qi_ref[0]
    w = w_ref[0] * (IDX_DIM ** -0.5 * IDX_HEADS ** -0.5)
    qb = qi.astype(BF16)
    for p in range(n_pages):
        d = _dot_nt(qb, pages[p][0, 0].astype(BF16))
        out_ref[0, p:p + 1, :] = jnp.sum(jnp.maximum(d, 0.0) * w, axis=0, keepdims=True)
    ki_new = kiwi_ref[0][:, :IDX_DIM].astype(BF16).astype(F32)
    d_new = jnp.sum(qi * ki_new, axis=-1, keepdims=True)
    s_new = jnp.sum(jnp.maximum(d_new, 0.0) * w, axis=0, keepdims=True)
    lane = lax.broadcasted_iota(I32, (1, LANES), 1)
    out_ref[0, n_pages:n_pages + 1, :] = jnp.where(lane == 0, s_new, -jnp.inf)


def _idx_sample(page_table, qi3, w3, kiwi3, cache_kidx, layer):
    nb, n_pages = page_table.shape
    page = lambda p: pl.BlockSpec((1, 1, PAGE_SIZE, IDX_DIM), lambda b, pt: (layer, pt[b, p], 0, 0))
    return pl.pallas_call(
        functools.partial(_idx_sample_kernel, n_pages=n_pages),
        grid_spec=pltpu.PrefetchScalarGridSpec(
            num_scalar_prefetch=1,
            grid=(nb,),
            in_specs=[pl.BlockSpec((1, IDX_HEADS, IDX_DIM), lambda b, pt: (b, 0, 0)),
                      pl.BlockSpec((1, IDX_HEADS, 1), lambda b, pt: (b, 0, 0)),
                      pl.BlockSpec((1, 1, LANES), lambda b, pt: (b, 0, 0))]
                     + [page(p) for p in range(n_pages)],
            out_specs=pl.BlockSpec((1, n_pages + 1, LANES), lambda b, pt: (b, 0, 0)),
        ),
        out_shape=jax.ShapeDtypeStruct((nb, n_pages + 1, LANES), F32),
        compiler_params=_params(("arbitrary",)),
        name="idx_sample",
    )(page_table, qi3, w3, kiwi3, *([cache_kidx] * n_pages))


def _select_sample_kernel(sc_ref, sel_ref, keys_ref, cut_ref, *, k):
    nc, rows, width = keys_ref.shape
    idx_bits = (nc * width - 1).bit_length()
    for c in range(nc):
        keys_ref[c] = _sortable(sc_ref[c])
    thr, cut = _select(keys_ref, cut_ref, nc, rows, width, k, idx_bits)
    col = lax.broadcasted_iota(I32, (rows, width), 1)
    for c in range(nc):
        x = keys_ref[c]
        sel = (x > thr) | ((x == thr) & (col + c * width <= cut))
        sel_ref[c] = jnp.where(sel, 1.0, 0.0)


def _select_sample(scores, k):
    nc, nb, width = scores.shape
    return pl.pallas_call(
        functools.partial(_select_sample_kernel, k=k),
        out_shape=jax.ShapeDtypeStruct((nc, nb, width), F32),
        scratch_shapes=[pltpu.VMEM((nc, nb, width), I32), pltpu.VMEM((nb, 1), I32)],
        compiler_params=pltpu.CompilerParams(vmem_limit_bytes=VMEM_LIMIT),
        name="select_sample",
    )(scores)


def _attn_sample_kernel(pt_ref, q_ref, kn_ref, vn_ref, g_ref, sel_ref, *refs, n_pages):
    kpages = refs[:n_pages]
    vpages = refs[n_pages:2 * n_pages]
    a_ref = refs[2 * n_pages]
    rows = PAGE_SIZE * A_KV_HEADS
    q = q_ref[0]
    qb = q.astype(BF16)
    sel = sel_ref[0]
    pos = lax.broadcasted_iota(I32, (PAGE_SIZE, rows), 0)
    r_e = lax.broadcasted_iota(I32, (PAGE_SIZE, rows), 1)
    expand = jnp.where((r_e >> 2) == pos, 1.0, 0.0).astype(BF16)
    selx = _dot(sel[:n_pages].astype(BF16), expand)
    head = lax.broadcasted_iota(I32, (A_HEADS, rows), 0)
    r_h = lax.broadcasted_iota(I32, (A_HEADS, rows), 1)
    head_ok = (r_h & (A_KV_HEADS - 1)) == (head >> 2)

    ss = []
    for p in range(n_pages):
        s = _dot_nt(qb, kpages[p][0, 0].astype(BF16))
        ok = head_ok & (selx[p:p + 1, :] > 0.5)
        ss.append(jnp.where(ok, s, -jnp.inf))
    kn = kn_ref[0].astype(BF16).astype(F32)
    s_new = jnp.sum(q * kn, axis=-1, keepdims=True)
    s_new = jnp.where(sel[n_pages:n_pages + 1, 0:1] > 0.5, s_new, -jnp.inf)
    m = s_new
    for s in ss:
        m = jnp.maximum(m, jnp.max(s, axis=-1, keepdims=True))
    p_new = jnp.exp(s_new - m)
    l = p_new
    vn = vn_ref[0].astype(BF16).astype(F32)
    o = p_new.astype(BF16).astype(F32) * vn
    for p in range(n_pages):
        e = jnp.exp(ss[p] - m)
        l = l + jnp.sum(e, axis=-1, keepdims=True)
        o = o + _dot(e.astype(BF16), vpages[p][0, 0].astype(BF16))
    a_ref[0] = (o / l) * _silu(g_ref[0])


def _attn_sample(page_table, q3, kn3, vn3, g3, sel, cache_k, cache_v, layer):
    nb, n_pages = page_table.shape
    rows = PAGE_SIZE * A_KV_HEADS
    tok = pl.BlockSpec((1, A_HEADS, A_HEAD_DIM), lambda b, pt: (b, 0, 0))
    page = lambda p: pl.BlockSpec((1, 1, rows, A_HEAD_DIM), lambda b, pt: (layer, pt[b, p], 0, 0))
    return pl.pallas_call(
        functools.partial(_attn_sample_kernel, n_pages=n_pages),
        grid_spec=pltpu.PrefetchScalarGridSpec(
            num_scalar_prefetch=1,
            grid=(nb,),
            in_specs=[tok, tok, tok, tok,
                      pl.BlockSpec((1, n_pages + 1, LANES), lambda b, pt: (b, 0, 0))]
                     + [page(p) for p in range(n_pages)] + [page(p) for p in range(n_pages)],
            out_specs=tok,
        ),
        out_shape=jax.ShapeDtypeStruct((nb, A_HEADS, A_HEAD_DIM), F32),
        compiler_params=_params(("arbitrary",)),
        name="attn_sample",
    )(page_table, q3, kn3, vn3, g3, sel, *([cache_k] * n_pages), *([cache_v] * n_pages))


def _log_decay():
    return jnp.log1p(-jnp.exp2(-5.0 - jnp.arange(R_HEADS, dtype=F32)))


def _group_norm_gate(o, gnw, g):
    mu = jnp.mean(o, axis=-1, keepdims=True)
    d = o - mu
    var = jnp.mean(d * d, axis=-1, keepdims=True)
    return d * lax.rsqrt(var + EPS) * gnw * _silu(g)


def _ret_prompt_kernel(q_ref, k_ref, v_ref, g_ref, dec_ref, rdec_ref, cdec_ref, sdec_ref, gnw_ref,
                       a_ref, s_ref):
    @pl.when(pl.program_id(1) == 0)
    def _():
        s_ref[...] = jnp.zeros(s_ref.shape, F32)

    for h in range(R_HEADS):
        qh = q_ref[:, h * R_KEY_DIM:(h + 1) * R_KEY_DIM]
        kh = k_ref[:, h * R_KEY_DIM:(h + 1) * R_KEY_DIM]
        vh = v_ref[:, h * R_VAL_DIM:(h + 1) * R_VAL_DIM]
        state = s_ref[0, h]
        qk = _dot_nt(qh, kh) * dec_ref[h]
        o = _dot(qk.astype(BF16), vh) + _dot(qh, state.astype(BF16)) * rdec_ref[h]
        kd = (kh.astype(F32) * cdec_ref[h]).T.astype(BF16)
        s_ref[0, h] = sdec_ref[h] * state + _dot(kd, vh)
        a = _group_norm_gate(o, gnw_ref[:, h * R_VAL_DIM:(h + 1) * R_VAL_DIM],
                             g_ref[:, h * R_VAL_DIM:(h + 1) * R_VAL_DIM])
        a_ref[:, h * R_VAL_DIM:(h + 1) * R_VAL_DIM] = a.astype(a_ref.dtype)


def _ret_prompt(q, k, v, g, gnw, batch, seq):
    nc = seq // R_CHUNK
    lg = _log_decay()
    i = jnp.arange(R_CHUNK, dtype=F32)
    diff = i[:, None] - i[None, :]
    dec = jnp.exp(jnp.where(diff[None] >= 0, diff[None] * lg[:, None, None], -jnp.inf))
    rdec = jnp.exp((i[None, :] + 1.0) * lg[:, None])[:, :, None]
    cdec = jnp.exp((R_CHUNK - 1.0 - i)[None, :] * lg[:, None])[:, :, None]
    sdec = jnp.broadcast_to(jnp.exp(R_CHUNK * lg)[:, None, None], (R_HEADS, 1, R_VAL_DIM))
    row = lambda n: pl.BlockSpec((R_CHUNK, n), lambda b, c: (b * nc + c, 0))
    const = lambda shape: pl.BlockSpec(shape, lambda b, c: (0,) * len(shape))
    return pl.pallas_call(
        _ret_prompt_kernel,
        grid=(batch, nc),
        in_specs=[row(R_QK_WIDTH), row(R_QK_WIDTH), row(R_WIDTH), row(R_WIDTH),
                  const(dec.shape), const(rdec.shape), const(cdec.shape), const(sdec.shape),
                  const((1, R_WIDTH))],
        out_specs=[row(R_WIDTH),
                   pl.BlockSpec((1, R_HEADS, R_KEY_DIM, R_VAL_DIM), lambda b, c: (b, 0, 0, 0))],
        out_shape=[jax.ShapeDtypeStruct((batch * seq, R_WIDTH), BF16),
                   jax.ShapeDtypeStruct((batch, R_HEADS, R_KEY_DIM, R_VAL_DIM), F32)],
        compiler_params=_params(("arbitrary", "arbitrary")),
        name="ret_prompt",
    )(q, k, v, g, dec, rdec, cdec, sdec, gnw)


def _ret_sample_kernel(s_ref, qc_ref, kc_ref, v_ref, g_ref, gnw_ref, dec_ref, a_ref, so_ref, *, tb):
    gamma = dec_ref[0]
    for j in range(tb):
        state = s_ref[0, j, 0]
        qc = qc_ref[0, 0][:, j:j + 1]
        kc = kc_ref[0, 0][:, j:j + 1]
        vr = v_ref[j:j + 1, :]
        so_ref[j, 0] = gamma * state + kc * vr
        qk = jnp.sum(qc * kc, axis=0, keepdims=True)
        o = qk * vr + jnp.sum(qc * state, axis=0, keepdims=True) * gamma
        a_ref[j:j + 1, :] = _group_norm_gate(o, gnw_ref[0], g_ref[j:j + 1, :])


def _ret_sample(state_ret, layer, qc, kc, v, g, gnw, tb):
    nb = v.shape[0]
    gamma = jnp.broadcast_to(jnp.exp(_log_decay())[:, None, None], (R_HEADS, 1, R_VAL_DIM))
    cols = pl.BlockSpec((1, 1, R_KEY_DIM, tb), lambda t, h: (t, h, 0, 0))
    vrow = pl.BlockSpec((tb, R_VAL_DIM), lambda t, h: (t, h))
    head = pl.BlockSpec((1, 1, R_VAL_DIM), lambda t, h: (h, 0, 0))
    return pl.pallas_call(
        functools.partial(_ret_sample_kernel, tb=tb),
        grid=(nb // tb, R_HEADS),
        in_specs=[pl.BlockSpec((1, tb, 1, R_KEY_DIM, R_VAL_DIM), lambda t, h: (layer, t, h, 0, 0)),
                  cols, cols, vrow, vrow, head, head],
        out_specs=[vrow, pl.BlockSpec((tb, 1, R_KEY_DIM, R_VAL_DIM), lambda t, h: (t, h, 0, 0))],
        out_shape=[jax.ShapeDtypeStruct((nb, R_WIDTH), F32),
                   jax.ShapeDtypeStruct((nb, R_HEADS, R_KEY_DIM, R_VAL_DIM), F32)],
        compiler_params=_params(("arbitrary", "arbitrary")),
        name="ret_sample",
    )(state_ret, qc, kc, v, g, gnw, gamma)


def _rope_tables(pos):
    half = R_KEY_DIM // 2
    theta = 1.0 / (ROPE_BASE ** jnp.linspace(0.0, 1.0, half, dtype=F32))
    ang = pos.astype(F32)[:, None] * theta[None, :]
    cos = jnp.repeat(jnp.cos(ang), 2, axis=-1)
    sin = jnp.repeat(jnp.sin(ang), 2, axis=-1)
    sign = jnp.tile(jnp.array([-1.0, 1.0], F32), half)
    return cos, sin * sign[None, :]


def _attn_layer(xp, xs, nw, w_in, w_out, cache_k, cache_v, cache_kidx, page_table, layer,
                batch, seq, final_w):
    w = jnp.pad(w_in, ((0, 0), (0, A_IN_PAD - A_IN))).astype(BF16)
    wo = w_out.astype(BF16)
    nb, n_pages = page_table.shape

    q, k, v, kb, vb, g, qi, kiwi = _attn_proj(xp, nw, w, 256)
    kib = kiwi[:, :IDX_DIM].astype(BF16)
    zeros = jnp.zeros_like(kib)
    klo = jnp.concatenate([kib, zeros], axis=1)
    khi = jnp.concatenate([zeros, kib], axis=1)
    a = _attn_prompt(q, g, qi, kiwi, kb, vb, klo, khi, batch, seq, 256)
    xp = _out_proj(a, wo, xp, 512, final_w)

    qs, ks, vs, _, _, gs, qis, kiwis = _attn_proj(xs, nw, w, nb)
    qi3 = qis.astype(F32).reshape(nb, IDX_HEADS, IDX_DIM)
    w3 = kiwis[:, IDX_DIM:IDX_DIM + IDX_HEADS].reshape(nb, IDX_HEADS, 1)
    scores = _idx_sample(page_table, qi3, w3, kiwis.reshape(nb, 1, LANES), cache_kidx, layer)
    n_sel = min(TOPK_MAX, (n_pages * PAGE_SIZE + 1) // 4)
    sel = _select_sample(scores.transpose(1, 0, 2), n_sel).transpose(1, 0, 2)
    rep = lambda t: jnp.repeat(t.reshape(nb, A_KV_HEADS, A_HEAD_DIM), A_GROUP, axis=1)
    a_s = _attn_sample(page_table, qs.astype(F32).reshape(nb, A_HEADS, A_HEAD_DIM), rep(ks), rep(vs),
                       gs.reshape(nb, A_HEADS, A_HEAD_DIM), sel, cache_k, cache_v, layer)
    xs = _out_proj(a_s.reshape(nb, A_WIDTH), wo, xs, nb, final_w)
    return xp, xs, (k, v, kiwi[:, :IDX_DIM]), (ks, vs, kiwis[:, :IDX_DIM])


def _ret_layer(xp, xs, nw, w_in, w_out, gnw, state_ret, layer, batch, seq, past, final_w):
    w = w_in.astype(BF16)
    wo = w_out.astype(BF16)
    gnw = gnw.reshape(1, R_WIDTH)
    nb = xs.shape[0]

    cos, sin = _rope_tables(jnp.arange(seq, dtype=I32))
    q, k, v, g = _ret_proj(xp, nw, w, cos, sin, 256, BF16)
    a, s_p = _ret_prompt(q, k, v, g, gnw, batch, seq)
    xp = _out_proj(a, wo, xp, 512, final_w)

    cos, sin = _rope_tables(jnp.full((nb,), past, I32))
    q, k, v, g = _ret_proj(xs, nw, w, cos, sin, nb, F32)
    tb = 8
    cols = lambda t: t.reshape(nb // tb, tb, R_HEADS, R_KEY_DIM).transpose(0, 2, 3, 1)
    a_s, s_s = _ret_sample(state_ret, layer, cols(q), cols(k), v, g, gnw.reshape(R_HEADS, 1, R_VAL_DIM), tb)
    xs = _out_proj(a_s, wo, xs, nb, final_w)
    return xp, xs, s_p, s_s


def kernel(x_prompt, x_sample, cache_k, cache_v, cache_kidx, state_ret, page_table, norm_w, w_in_attn,
           w_out_attn, w_in_ret, w_out_ret, gn_w_ret, final_norm_w):
    batch, seq, _ = x_prompt.shape
    nb = x_sample.shape[0]
    n_pool = cache_k.shape[1]
    past = page_table.shape[1] * PAGE_SIZE
    xp = x_prompt.reshape(batch * seq, D_MODEL)
    xs = x_sample.reshape(nb, D_MODEL)
    ck = cache_k.reshape(cache_k.shape[0], n_pool, PAGE_SIZE * A_KV_HEADS, A_HEAD_DIM)
    cv = cache_v.reshape(cache_v.shape[0], n_pool, PAGE_SIZE * A_KV_HEADS, A_HEAD_DIM)
    fw = final_norm_w.reshape(1, D_MODEL)

    kv_p, kv_s, st_p, st_s = [], [], [], []
    for i in range(DEPTH):
        nw = norm_w[i].reshape(1, D_MODEL)
        last = fw if i == DEPTH - 1 else None
        j = i // 2
        if i % 2 == 0:
            xp, xs, new_p, new_s = _attn_layer(xp, xs, nw, w_in_attn[j], w_out_attn[j], ck, cv, cache_kidx,
                                               page_table, j, batch, seq, last)
            kv_p.append(new_p)
            kv_s.append(new_s)
        else:
            xp, xs, s_p, s_s = _ret_layer(xp, xs, nw, w_in_ret[j], w_out_ret[j], gn_w_ret[j], state_ret, j,
                                          batch, seq, past, last)
            st_p.append(s_p)
            st_s.append(s_s)

    kvh = (A_KV_HEADS, A_HEAD_DIM)
    return (xp.reshape(batch, seq, D_MODEL),
            xs.reshape(nb, 1, D_MODEL),
            jnp.stack([t[0].reshape(batch, seq, *kvh) for t in kv_p]),
            jnp.stack([t[1].reshape(batch, seq, *kvh) for t in kv_p]),
            jnp.stack([t[2].reshape(batch, seq, IDX_DIM) for t in kv_p]),
            jnp.stack([t[0].reshape(nb, 1, *kvh) for t in kv_s]),
            jnp.stack([t[1].reshape(nb, 1, *kvh) for t in kv_s]),
            jnp.stack([t[2].reshape(nb, 1, IDX_DIM) for t in kv_s]),
            jnp.stack(st_p),
            jnp.stack(st_s))
```
